```python
import jax, jax.numpy as jnp
from jax import lax
import numpy as np

D_MODEL = 1024
BATCH = 16
SEQ = 2048
DEPTH = 1

N_HEADS = 8
N_KV_HEADS = 2
HEAD_DIM = 64
WINDOW = 128
ATTN_BLOCK = 128
GMLP_WIDTH = 512
GMLP_GROUPS = 4
GMLP_CHUNK = 128
N_EXPERTS = 32
TOP_K = 4
D_FF_EXPERT = D_MODEL
SWIGLU_LIMIT = 7.0
SWIGLU_ALPHA = 1.702
MOE_BLOCK = 128
PLE_DIM = 256

RMS_EPS = 1e-6
LN_EPS = 1e-5

Q_WIDTH = N_HEADS * HEAD_DIM
KV_WIDTH = N_KV_HEADS * HEAD_DIM
IN_WIDTH = Q_WIDTH + 2 * KV_WIDTH + 2 * GMLP_WIDTH + 2 * D_MODEL
IN_SPLITS = list(np.cumsum([Q_WIDTH, KV_WIDTH, KV_WIDTH, GMLP_WIDTH, GMLP_WIDTH, D_MODEL]).tolist())

kernel_name = "hybrid_swa_gmlp_moe_block"


def rmsnorm(x, g):
    xf = x.astype(jnp.float32)
    y = xf * lax.rsqrt(jnp.mean(xf * xf, axis=-1, keepdims=True) + RMS_EPS) * g.astype(jnp.float32)
    return y.astype(x.dtype)


def layernorm(x, g, b):
    xf = x.astype(jnp.float32)
    mu = jnp.mean(xf, axis=-1, keepdims=True)
    xc = xf - mu
    var = jnp.mean(xc * xc, axis=-1, keepdims=True)
    y = xc * lax.rsqrt(var + LN_EPS) * g.astype(jnp.float32) + b.astype(jnp.float32)
    return y.astype(x.dtype)


def alibi_slopes(n_heads):
    return jnp.asarray(2.0 ** (-8.0 * (np.arange(n_heads, dtype=np.float32) + 1.0) / n_heads), dtype=jnp.float32)


def sliding_window_attention(q, k, v, sinks):
    B, S = q.shape[0], q.shape[1]
    nb = S // ATTN_BLOCK
    G = N_HEADS // N_KV_HEADS
    qb = q.reshape(B, nb, ATTN_BLOCK, N_KV_HEADS, G, HEAD_DIM)
    kb = k.reshape(B, nb, ATTN_BLOCK, N_KV_HEADS, HEAD_DIM)
    vb = v.reshape(B, nb, ATTN_BLOCK, N_KV_HEADS, HEAD_DIM)
    pad = ((0, 0), (1, 0), (0, 0), (0, 0), (0, 0))
    kk = jnp.concatenate([jnp.pad(kb, pad)[:, :-1], kb], axis=2)
    vv = jnp.concatenate([jnp.pad(vb, pad)[:, :-1], vb], axis=2)
    scores = jnp.einsum('bnqkgd,bnskd->bnkgqs', qb, kk).astype(jnp.float32) * (HEAD_DIM ** -0.5)
    qi = jnp.arange(ATTN_BLOCK)[:, None]
    sj = jnp.arange(2 * ATTN_BLOCK)[None, :]
    dist = (qi + ATTN_BLOCK - sj).astype(jnp.float32)
    in_window = (dist >= 0) & (dist < WINDOW)
    key_pos = (jnp.arange(nb)[:, None] - 1) * ATTN_BLOCK + jnp.arange(2 * ATTN_BLOCK)[None, :]
    mask = in_window[None, :, :] & (key_pos >= 0)[:, None, :]
    slopes = alibi_slopes(N_HEADS).reshape(N_KV_HEADS, G)
    scores = scores - slopes[:, :, None, None] * dist[None, None, :, :]
    scores = jnp.where(mask[None, :, None, None, :, :], scores, -jnp.inf)
    sink_col = jnp.broadcast_to(sinks.astype(jnp.float32).reshape(1, 1, N_KV_HEADS, G, 1, 1),
                                scores.shape[:-1] + (1,))
    probs = jax.nn.softmax(jnp.concatenate([scores, sink_col], axis=-1), axis=-1)[..., :-1]
    out = jnp.einsum('bnkgqs,bnskd->bnqkgd', probs.astype(v.dtype), vv)
    return out.reshape(B, S, Q_WIDTH)


def chunked_spatial_gating(u, v, w_spatial, b_spatial, g_ln, b_ln):
    B, S = u.shape[0], u.shape[1]
    nc = S // GMLP_CHUNK
    cg = GMLP_WIDTH // GMLP_GROUPS
    v = layernorm(v, g_ln, b_ln)
    vc = v.reshape(B, nc, GMLP_CHUNK, GMLP_GROUPS, cg)
    causal = jnp.tril(jnp.ones((GMLP_CHUNK, GMLP_CHUNK), dtype=w_spatial.dtype))
    w = w_spatial * causal[None]
    mixed = jnp.einsum('gts,bnsgc->bntgc', w, vc) + b_spatial.T[None, None, :, :, None]
    return u * mixed.reshape(B, S, GMLP_WIDTH)


def clamped_swiglu(a, b):
    a = jnp.minimum(a, SWIGLU_LIMIT)
    b = jnp.clip(b, -SWIGLU_LIMIT, SWIGLU_LIMIT)
    return (b + 1.0) * (a * jax.nn.sigmoid(SWIGLU_ALPHA * a))


def moe(h, w_router, b_router, w_gate, b_gate, w_up, b_up, w_down, b_down):
    B, S, D = h.shape
    T = B * S
    hf = h.reshape(T, D)
    logits = (hf @ w_router + b_router).astype(jnp.float32)
    top_val, top_idx = lax.top_k(logits, TOP_K)
    gates = jax.nn.softmax(top_val, axis=-1)
    flat_e = top_idx.reshape(-1).astype(jnp.int32)
    flat_g = gates.reshape(-1)
    order = jnp.argsort(flat_e)
    sorted_e = flat_e[order]
    sorted_tok = (order // TOP_K).astype(jnp.int32)
    sorted_g = flat_g[order]
    counts = jnp.bincount(flat_e, length=N_EXPERTS).astype(jnp.int32)
    padded = (counts + MOE_BLOCK - 1) // MOE_BLOCK * MOE_BLOCK
    pend = jnp.cumsum(padded)
    pstart = pend - padded
    ustart = jnp.cumsum(counts) - counts
    dest = pstart[sorted_e] + jnp.arange(T * TOP_K, dtype=jnp.int32) - ustart[sorted_e]
    capacity = T * TOP_K + N_EXPERTS * MOE_BLOCK
    n_blocks = capacity // MOE_BLOCK
    row_tok = jnp.full((capacity,), T, jnp.int32).at[dest].set(sorted_tok)
    row_g = jnp.zeros((capacity,), jnp.float32).at[dest].set(sorted_g)
    block_e = jnp.minimum(jnp.searchsorted(pend, jnp.arange(n_blocks, dtype=jnp.int32) * MOE_BLOCK, side='right'),
                          N_EXPERTS - 1).astype(jnp.int32)
    h_pad = jnp.concatenate([hf, jnp.zeros((1, D), hf.dtype)], axis=0)

    def expert_block(args):
        tok, g, e = args
        xb = h_pad[tok]
        a = xb @ w_gate[e] + b_gate[e]
        b = xb @ w_up[e] + b_up[e]
        y = clamped_swiglu(a, b) @ w_down[e] + b_down[e]
        return (y * g[:, None]).astype(h.dtype)

    ys = lax.map(expert_block, (row_tok.reshape(n_blocks, MOE_BLOCK), row_g.reshape(n_blocks, MOE_BLOCK), block_e))
    out = jnp.zeros((T + 1, D), h.dtype).at[row_tok].add(ys.reshape(capacity, D))
    return out[:T].reshape(B, S, D)


def setup_inputs(seed: int = 0) -> dict:
    key = jax.random.key(seed)
    ks = jax.random.split(key, 32)
    f32 = jnp.float32
    L, D, E, F = DEPTH, D_MODEL, N_EXPERTS, D_FF_EXPERT

    def nrm(k, shape, scale):
        return jax.random.normal(k, shape, f32) * scale

    return {
        "x": nrm(ks[0], (BATCH, SEQ, D), 1.0),
        "p": nrm(ks[1], (DEPTH, BATCH, SEQ, PLE_DIM), 1.0),
        "g_mix": 1.0 + nrm(ks[2], (L, D), 0.02),
        "w_in": nrm(ks[3], (L, D, IN_WIDTH), D ** -0.5),
        "attn_sinks": nrm(ks[4], (L, N_HEADS), 1.0),
        "g_sgu": 1.0 + nrm(ks[5], (L, GMLP_WIDTH), 0.02),
        "b_sgu": nrm(ks[6], (L, GMLP_WIDTH), 0.02),
        "w_spatial": nrm(ks[7], (L, GMLP_GROUPS, GMLP_CHUNK, GMLP_CHUNK), GMLP_CHUNK ** -0.5),
        "b_spatial": nrm(ks[8], (L, GMLP_GROUPS, GMLP_CHUNK), 0.02),
        "w_attn_proj": nrm(ks[9], (L, Q_WIDTH, D), Q_WIDTH ** -0.5),
        "w_sgu_proj": nrm(ks[10], (L, GMLP_WIDTH, D), GMLP_WIDTH ** -0.5),
        "w_out": nrm(ks[11], (L, D, D), D ** -0.5),
        "g_ffn": 1.0 + nrm(ks[12], (L, D), 0.02),
        "w_router": nrm(ks[13], (L, D, E), D ** -0.5),
        "b_router": nrm(ks[14], (L, E), 0.01),
        "w_gate": nrm(ks[15], (L, E, D, F), D ** -0.5),
        "b_gate": nrm(ks[16], (L, E, F), 0.01),
        "w_up": nrm(ks[17], (L, E, D, F), D ** -0.5),
        "b_up": nrm(ks[18], (L, E, F), 0.01),
        "w_down": nrm(ks[19], (L, E, F, D), F ** -0.5),
        "b_down": nrm(ks[20], (L, E, D), 0.01),
        "g_ple": 1.0 + nrm(ks[21], (L, D), 0.02),
        "w_ple_gate": nrm(ks[22], (L, D, D), D ** -0.5),
        "w_ple_proj": nrm(ks[23], (L, PLE_DIM, D), PLE_DIM ** -0.5),
        "g_final": 1.0 + nrm(ks[24], (D,), 0.02),
    }


def reference(x, p, g_mix, w_in, attn_sinks, g_sgu, b_sgu, w_spatial, b_spatial, w_attn_proj, w_sgu_proj,
              w_out, g_ffn, w_router, b_router, w_gate, b_gate, w_up, b_up, w_down, b_down,
              g_ple, w_ple_gate, w_ple_proj, g_final):
    B, S = x.shape[0], x.shape[1]
    for i in range(DEPTH):
        h = rmsnorm(x, g_mix[i])
        z = h @ w_in[i]
        q, k, v, gu, gv, ga, gb = jnp.split(z, IN_SPLITS, axis=-1)
        attn = sliding_window_attention(q.reshape(B, S, N_HEADS, HEAD_DIM),
                                        k.reshape(B, S, N_KV_HEADS, HEAD_DIM),
                                        v.reshape(B, S, N_KV_HEADS, HEAD_DIM),
                                        attn_sinks[i])
        sgu = chunked_spatial_gating(jax.nn.gelu(gu), jax.nn.gelu(gv), w_spatial[i], b_spatial[i],
                                     g_sgu[i], b_sgu[i])
        merged = jax.nn.sigmoid(ga) * (attn @ w_attn_proj[i]) + jax.nn.sigmoid(gb) * (sgu @ w_sgu_proj[i])
        x = x + merged @ w_out[i]
        x = x + moe(rmsnorm(x, g_ffn[i]), w_router[i], b_router[i], w_gate[i], b_gate[i],
                    w_up[i], b_up[i], w_down[i], b_down[i])
        hp = rmsnorm(x, g_ple[i])
        x = x + jax.nn.sigmoid(hp @ w_ple_gate[i]) * (p[i] @ w_ple_proj[i])
    return rmsnorm(x, g_final)
```

```python
import functools

import numpy as np
import jax
import jax.numpy as jnp
from jax import lax
from jax.experimental import pallas as pl
from jax.experimental.pallas import tpu as pltpu

D_MODEL = 1024
N_HEADS = 8
N_KV_HEADS = 2
HEAD_DIM = 64
WINDOW = 128
ATTN_BLOCK = 128
GMLP_WIDTH = 512
GMLP_GROUPS = 4
GMLP_CHUNK = 128
N_EXPERTS = 32
TOP_K = 4
SWIGLU_LIMIT = 7.0
SWIGLU_ALPHA = 1.702
PLE_DIM = 256
RMS_EPS = 1e-6
LN_EPS = 1e-5

Q_WIDTH = N_HEADS * HEAD_DIM
KV_WIDTH = N_KV_HEADS * HEAD_DIM
GROUP = N_HEADS // N_KV_HEADS

LANES = 128
SUBLANES = 8
ROW_CHUNKS = D_MODEL // LANES

OFF_Q = 0
OFF_K = OFF_Q + Q_WIDTH
OFF_V = OFF_K + KV_WIDTH
OFF_GU = OFF_V + KV_WIDTH
OFF_GV = OFF_GU + GMLP_WIDTH
OFF_GA = OFF_GV + GMLP_WIDTH
OFF_GB = OFF_GA + D_MODEL
IN_WIDTH = OFF_GB + D_MODEL

MIX_TM = 512
EXP_TM = 256
CMB_TM = 256
ISSUE_UNROLL = 8
VMEM_LIMIT = 56 * 1024 * 1024

_F32 = jnp.float32
_BF16 = jnp.bfloat16


def _dot(a, b):
    return jnp.dot(a, b, preferred_element_type=_F32)


def _rms(x, g):
    return x * lax.rsqrt(jnp.mean(x * x, axis=-1, keepdims=True) + RMS_EPS) * g


def _store_rows_as_tiles(ref, val, n_rows):
    for c in range(ROW_CHUNKS):
        ref[pl.ds(c, n_rows, stride=ROW_CHUNKS), :] = val[:, c * LANES:(c + 1) * LANES]


def _load_tiles_as_rows(ref, base, n_rows):
    return jnp.concatenate(
        [ref[pl.ds(base + c, n_rows, stride=ROW_CHUNKS), :] for c in range(ROW_CHUNKS)], axis=1)


def _mixer_kernel(x_ref, gmix_ref, win_ref, bias_ref, sink_ref, gsgu_ref, bsgu_ref, wsp_ref, bsp_ref,
                  wa_ref, wb_ref, wout_ref, gffn_ref, wrh_ref, wrl_ref, br_ref, tri_ref,
                  x1_ref, h2_ref, idx_ref, gate_ref, pos_ref, cnt_ref,
                  kprev_ref, vprev_ref, carry_ref):
    b = pl.program_id(0)
    s = pl.program_id(1)
    tm = x_ref.shape[1]

    @pl.when(s == 0)
    def _():
        kprev_ref[...] = jnp.zeros_like(kprev_ref)
        vprev_ref[...] = jnp.zeros_like(vprev_ref)

    @pl.when((b == 0) & (s == 0))
    def _():
        carry_ref[...] = jnp.zeros_like(carry_ref)

    x = x_ref[0]
    hb = _rms(x, gmix_ref[...]).astype(_BF16)

    qkv = _dot(hb, win_ref[:, OFF_Q:OFF_GU])
    lane = lax.broadcasted_iota(jnp.int32, (ATTN_BLOCK, LANES), 1)
    low = lane < HEAD_DIM
    sink = sink_ref[...]
    first = jnp.where(s == 0, 1, 0)
    kp = kprev_ref[...]
    vp = vprev_ref[...]
    attn_blocks = []
    for n in range(tm // ATTN_BLOCK):
        r0 = n * ATTN_BLOCK
        kc = qkv[r0:r0 + ATTN_BLOCK, OFF_K:OFF_V].astype(_BF16)
        vc = qkv[r0:r0 + ATTN_BLOCK, OFF_V:OFF_GU].astype(_BF16)
        pairs = [qkv[r0:r0 + ATTN_BLOCK, g * LANES:(g + 1) * LANES] * (HEAD_DIM ** -0.5) for g in range(GROUP)]
        qs = jnp.concatenate([jnp.where(low, t, 0.0) for t in pairs] +
                             [jnp.where(low, 0.0, t) for t in pairs], axis=0).astype(_BF16)
        kk = jnp.concatenate([kp, kc], axis=0)
        vv = jnp.concatenate([vp, vc], axis=0)
        sc = lax.dot_general(qs, kk, (((1,), (1,)), ((), ())), preferred_element_type=_F32)
        sc = sc + (bias_ref[first] if n == 0 else bias_ref[0])
        m = jnp.maximum(jnp.max(sc, axis=-1, keepdims=True), sink)
        e = jnp.exp(sc - m)
        den = jnp.sum(e, axis=-1, keepdims=True) + jnp.exp(sink - m)
        o = _dot((e / den).astype(_BF16), vv)
        attn_blocks.append(jnp.concatenate(
            [jnp.where(low, o[g * ATTN_BLOCK:(g + 1) * ATTN_BLOCK], o[(GROUP + g) * ATTN_BLOCK:(GROUP + g + 1) * ATTN_BLOCK])
             for g in range(GROUP)], axis=1))
        kp, vp = kc, vc
    kprev_ref[...] = kp
    vprev_ref[...] = vp
    attn = jnp.concatenate(attn_blocks, axis=0).astype(_BF16)

    guv = _dot(hb, win_ref[:, OFF_GU:OFF_GA])
    gu = jax.nn.gelu(guv[:, :GMLP_WIDTH])
    gv = jax.nn.gelu(guv[:, GMLP_WIDTH:])
    mu = jnp.mean(gv, axis=-1, keepdims=True)
    gc = gv - mu
    var = jnp.mean(gc * gc, axis=-1, keepdims=True)
    vln = (gc * lax.rsqrt(var + LN_EPS) * gsgu_ref[...] + bsgu_ref[...]).astype(_BF16)
    trow = lax.broadcasted_iota(jnp.int32, (GMLP_CHUNK, GMLP_CHUNK), 0)
    tcol = lax.broadcasted_iota(jnp.int32, (GMLP_CHUNK, GMLP_CHUNK), 1)
    wms = [jnp.where(trow >= tcol, wsp_ref[g], 0.0).astype(_BF16) for g in range(GMLP_GROUPS)]
    cg = GMLP_WIDTH // GMLP_GROUPS
    mixed_rows = []
    for c in range(tm // GMLP_CHUNK):
        r0 = c * GMLP_CHUNK
        mixed_rows.append(jnp.concatenate(
            [_dot(wms[g], vln[r0:r0 + GMLP_CHUNK, g * cg:(g + 1) * cg]) for g in range(GMLP_GROUPS)], axis=1)
            + bsp_ref[...])
    sgu = (gu * jnp.concatenate(mixed_rows, axis=0)).astype(_BF16)

    ga = _dot(hb, win_ref[:, OFF_GA:OFF_GB])
    gb = _dot(hb, win_ref[:, OFF_GB:IN_WIDTH])
    merged = jax.nn.sigmoid(ga) * _dot(attn, wa_ref[...]) + jax.nn.sigmoid(gb) * _dot(sgu, wb_ref[...])
    x1 = x + _dot(merged.astype(_BF16), wout_ref[...])
    x1_ref[0] = x1

    h2 = _rms(x1, gffn_ref[...])
    _store_rows_as_tiles(h2_ref, h2, tm)
    h_hi = h2.astype(_BF16)
    h_lo = (h2 - h_hi.astype(_F32)).astype(_BF16)
    logits = _dot(h_hi, wrh_ref[...]) + _dot(h_hi, wrl_ref[...]) + _dot(h_lo, wrh_ref[...])
    lt = logits.T[0:N_EXPERTS, :] + br_ref[...]
    eio = lax.broadcasted_iota(jnp.int32, (N_EXPERTS, tm), 0)
    vals, idxs = [], []
    for _ in range(TOP_K):
        mk = jnp.max(lt, axis=0, keepdims=True)
        ik = jnp.min(jnp.where(lt == mk, eio, N_EXPERTS), axis=0, keepdims=True)
        vals.append(mk)
        idxs.append(ik)
        lt = jnp.where(eio == ik, -jnp.inf, lt)
    ex = [jnp.exp(v - vals[0]) for v in vals]
    den = ex[0] + ex[1] + ex[2] + ex[3]
    gate_ref[...] = jnp.concatenate([t / den for t in ex], axis=0)
    idx_ref[...] = jnp.concatenate(idxs, axis=0)

    onehots = [eio == ik for ik in idxs]
    sel = (onehots[0] | onehots[1] | onehots[2] | onehots[3]).astype(_F32)
    base = _dot(sel.astype(_BF16), tri_ref[...]) + carry_ref[:, 0:1]
    pos_ref[...] = jnp.concatenate(
        [jnp.sum(jnp.where(oh, base, 0.0), axis=0, keepdims=True) for oh in onehots], axis=0).astype(jnp.int32)
    carry = carry_ref[...] + jnp.sum(sel, axis=1, keepdims=True)
    carry_ref[...] = carry
    cnt_ref[...] = carry


def _attention_bias_table():
    qi = np.arange(ATTN_BLOCK)[:, None]
    sj = np.arange(2 * ATTN_BLOCK)[None, :]
    dist = (qi + ATTN_BLOCK - sj).astype(np.float32)
    in_window = (dist >= 0) & (dist < WINDOW)
    slopes = (2.0 ** (-8.0 * (np.arange(N_HEADS, dtype=np.float32) + 1.0) / N_HEADS)).astype(np.float32)
    bias = -slopes[:, None, None] * dist[None]
    neg = np.float32(-np.inf)
    full = np.where(in_window[None], bias, neg)
    start = np.where((in_window & (sj >= ATTN_BLOCK))[None], bias, neg)
    tab = np.stack([full, start]).reshape(2, N_HEADS * ATTN_BLOCK, 2 * ATTN_BLOCK)
    return jnp.asarray(tab, dtype=_F32)


def _pair_perm():
    idx = []
    for g in range(GROUP):
        idx += list(range(g * HEAD_DIM, (g + 1) * HEAD_DIM))
        idx += list(range((GROUP + g) * HEAD_DIM, (GROUP + g + 1) * HEAD_DIM))
    return np.asarray(idx, dtype=np.int32)


def _const_spec(shape):
    nd = len(shape)
    return pl.BlockSpec(shape, lambda *_: (0,) * nd)


def _mixer(x, g_mix, w_in, attn_sinks, g_sgu, b_sgu, w_spatial, b_spatial, w_attn_proj, w_sgu_proj, w_out,
           g_ffn, w_router, b_router):
    B, S, D = x.shape
    T = B * S
    tm = MIX_TM
    n_s = S // tm
    perm = _pair_perm()
    win = jnp.concatenate([w_in[:, perm], w_in[:, Q_WIDTH:]], axis=1).astype(_BF16)
    wa = w_attn_proj[perm, :].astype(_BF16)
    sink_col = jnp.repeat(attn_sinks.astype(_F32), ATTN_BLOCK)[:, None]
    bsp_tile = jnp.repeat(b_spatial.T, GMLP_WIDTH // GMLP_GROUPS, axis=1)
    wr = jnp.pad(w_router, ((0, 0), (0, LANES - N_EXPERTS)))
    wr_hi = wr.astype(_BF16)
    wr_lo = (wr - wr_hi.astype(_F32)).astype(_BF16)
    tri = jnp.asarray(np.triu(np.ones((tm, tm), np.float32), k=1), dtype=_BF16)
    bias_tab = _attention_bias_table()

    operands = [
        (x, pl.BlockSpec((1, tm, D), lambda b, s: (b, s, 0))),
        (g_mix.reshape(1, D), _const_spec((1, D))),
        (win, _const_spec((D, IN_WIDTH))),
        (bias_tab, _const_spec(bias_tab.shape)),
        (sink_col, _const_spec(sink_col.shape)),
        (g_sgu.reshape(1, GMLP_WIDTH), _const_spec((1, GMLP_WIDTH))),
        (b_sgu.reshape(1, GMLP_WIDTH), _const_spec((1, GMLP_WIDTH))),
        (w_spatial, _const_spec(w_spatial.shape)),
        (bsp_tile, _const_spec(bsp_tile.shape)),
        (wa, _const_spec(wa.shape)),
        (w_sgu_proj.astype(_BF16), _const_spec(w_sgu_proj.shape)),
        (w_out.astype(_BF16), _const_spec(w_out.shape)),
        (g_ffn.reshape(1, D), _const_spec((1, D))),
        (wr_hi, _const_spec(wr_hi.shape)),
        (wr_lo, _const_spec(wr_lo.shape)),
        (b_router.reshape(N_EXPERTS, 1), _const_spec((N_EXPERTS, 1))),
        (tri, _const_spec(tri.shape)),
    ]
    tok_map = lambda b, s: (0, b * n_s + s)
    out_shape = (
        jax.ShapeDtypeStruct((B, S, D), _F32),
        jax.ShapeDtypeStruct((T * ROW_CHUNKS, LANES), _F32),
        jax.ShapeDtypeStruct((TOP_K, T), jnp.int32),
        jax.ShapeDtypeStruct((TOP_K, T), _F32),
        jax.ShapeDtypeStruct((TOP_K, T), jnp.int32),
        jax.ShapeDtypeStruct((N_EXPERTS, LANES), _F32),
    )
    out_specs = (
        pl.BlockSpec((1, tm, D), lambda b, s: (b, s, 0)),
        pl.BlockSpec((tm * ROW_CHUNKS, LANES), lambda b, s: (b * n_s + s, 0)),
        pl.BlockSpec((TOP_K, tm), tok_map),
        pl.BlockSpec((TOP_K, tm), tok_map),
        pl.BlockSpec((TOP_K, tm), tok_map),
        _const_spec((N_EXPERTS, LANES)),
    )
    return pl.pallas_call(
        _mixer_kernel,
        grid=(B, n_s),
        in_specs=[spec for _, spec in operands],
        out_specs=out_specs,
        out_shape=out_shape,
        scratch_shapes=[
            pltpu.VMEM((ATTN_BLOCK, KV_WIDTH), _BF16),
            pltpu.VMEM((ATTN_BLOCK, KV_WIDTH), _BF16),
            pltpu.VMEM((N_EXPERTS, LANES), _F32),
        ],
        compiler_params=pltpu.CompilerParams(
            dimension_semantics=("arbitrary", "arbitrary"), vmem_limit_bytes=VMEM_LIMIT),
        name="mixer",
    )(*[a for a, _ in operands])


def _row_gather(idx_ref, idx_base, src_hbm, dst_ref, dst_base, n_rows, sem):
    def body(j, carry):
        for u in range(ISSUE_UNROLL):
            r = j * ISSUE_UNROLL + u
            row = idx_ref[idx_base + r]
            pltpu.make_async_copy(
                src_hbm.at[row],
                dst_ref.at[pl.ds(pl.multiple_of((dst_base + r) * ROW_CHUNKS, ROW_CHUNKS), ROW_CHUNKS), :],
                sem).start()
        return carry

    lax.fori_loop(0, n_rows // ISSUE_UNROLL, body, 0)


def _wait_rows(buf_ref, base_tile, n_rows, sem):
    view = buf_ref.at[pl.ds(pl.multiple_of(base_tile * ROW_CHUNKS, ROW_CHUNKS), n_rows * ROW_CHUNKS), :]
    pltpu.make_async_copy(view, view, sem).wait()


def _experts_kernel(be_ref, nu_ref, tok_hbm, h2_hbm, wg_ref, bg_ref, wu_ref, bu_ref, wd_ref, bd_ref,
                    y_ref, buf_ref, tok_smem, row_sem, tok_sem):
    i = pl.program_id(0)
    tm = EXP_TM
    n_used = nu_ref[0]
    slot = lax.rem(i, 2)
    nxt = 1 - slot

    def tok_copy(block, sl):
        return pltpu.make_async_copy(tok_hbm.at[pl.ds(pl.multiple_of(block * tm, tm), tm)],
                                     tok_smem.at[pl.ds(pl.multiple_of(sl * tm, tm), tm)], tok_sem.at[sl])

    @pl.when((i == 0) & (n_used > 0))
    def _():
        tok_copy(0, 0).start()
        tok_copy(0, 0).wait()
        _row_gather(tok_smem, 0, h2_hbm, buf_ref, 0, tm, row_sem.at[0])

    @pl.when((i == 0) & (n_used > 1))
    def _():
        tok_copy(1, 1).start()

    @pl.when(i + 1 < n_used)
    def _():
        tok_copy(i + 1, nxt).wait()
        _row_gather(tok_smem, nxt * tm, h2_hbm, buf_ref, nxt * tm, tm, row_sem.at[nxt])

    @pl.when(i + 2 < n_used)
    def _():
        tok_copy(i + 2, slot).start()

    @pl.when(i < n_used)
    def _():
        _wait_rows(buf_ref, slot * tm, tm, row_sem.at[slot])
        xs = _load_tiles_as_rows(buf_ref, slot * tm * ROW_CHUNKS, tm).astype(_BF16)
        a = _dot(xs, wg_ref[0]) + bg_ref[0]
        bb = _dot(xs, wu_ref[0]) + bu_ref[0]
        a = jnp.minimum(a, SWIGLU_LIMIT)
        bb = jnp.clip(bb, -SWIGLU_LIMIT, SWIGLU_LIMIT)
        hmid = ((bb + 1.0) * (a * jax.nn.sigmoid(SWIGLU_ALPHA * a))).astype(_BF16)
        y = _dot(hmid, wd_ref[0]) + bd_ref[0]
        _store_rows_as_tiles(y_ref, y, tm)

    @pl.when(i >= n_used)
    def _():
        y_ref[...] = jnp.zeros_like(y_ref)


def _experts(block_e, n_used, row_tok, h2_tiles, w_gate, b_gate, w_up, b_up, w_down, b_down, n_blocks):
    tm = EXP_TM
    E, D, F = w_gate.shape
    w_spec = lambda shape: pl.BlockSpec(shape, lambda i, be, nu: (be[i], 0, 0))
    grid_spec = pltpu.PrefetchScalarGridSpec(
        num_scalar_prefetch=2,
        grid=(n_blocks,),
        in_specs=[
            pl.BlockSpec(memory_space=pl.ANY),
            pl.BlockSpec(memory_space=pl.ANY),
            w_spec((1, D, F)), w_spec((1, 1, F)),
            w_spec((1, D, F)), w_spec((1, 1, F)),
            w_spec((1, F, D)), w_spec((1, 1, D)),
        ],
        out_specs=pl.BlockSpec((tm * ROW_CHUNKS, LANES), lambda i, be, nu: (i, 0)),
        scratch_shapes=[
            pltpu.VMEM((2 * tm * ROW_CHUNKS, LANES), _F32),
            pltpu.SMEM((2 * tm,), jnp.int32),
            pltpu.SemaphoreType.DMA((2,)),
            pltpu.SemaphoreType.DMA((2,)),
        ],
    )
    return pl.pallas_call(
        _experts_kernel,
        grid_spec=grid_spec,
        out_shape=jax.ShapeDtypeStruct((n_blocks * tm * ROW_CHUNKS, LANES), _F32),
        compiler_params=pltpu.CompilerParams(
            dimension_semantics=("arbitrary",), vmem_limit_bytes=VMEM_LIMIT),
        name="experts",
    )(block_e, n_used, row_tok, h2_tiles,
      w_gate.astype(_BF16), b_gate.reshape(E, 1, F), w_up.astype(_BF16), b_up.reshape(E, 1, F),
      w_down.astype(_BF16), b_down.reshape(E, 1, D))


def _combine_kernel(dest_hbm, y_hbm, x1_ref, gate_ref, p_ref, gple_ref, wpg_ref, wpp_ref, gfin_ref,
                    out_ref, buf_ref, dest_smem, row_sem, dest_sem):
    i = pl.program_id(0)
    n = pl.num_programs(0)
    tm = CMB_TM
    rows = TOP_K * tm
    slot = lax.rem(i, 2)
    nxt = 1 - slot

    def dest_copy(block, sl):
        return pltpu.make_async_copy(dest_hbm.at[pl.ds(pl.multiple_of(block * rows, rows), rows)],
                                     dest_smem.at[pl.ds(pl.multiple_of(sl * rows, rows), rows)], dest_sem.at[sl])

    @pl.when(i == 0)
    def _():
        dest_copy(0, 0).start()
        dest_copy(0, 0).wait()
        _row_gather(dest_smem, 0, y_hbm, buf_ref, 0, rows, row_sem.at[0])

    @pl.when((i == 0) & (n > 1))
    def _():
        dest_copy(1, 1).start()

    @pl.when(i + 1 < n)
    def _():
        dest_copy(i + 1, nxt).wait()
        _row_gather(dest_smem, nxt * rows, y_hbm, buf_ref, nxt * rows, rows, row_sem.at[nxt])

    @pl.when(i + 2 < n)
    def _():
        dest_copy(i + 2, slot).start()

    _wait_rows(buf_ref, slot * rows, rows, row_sem.at[slot])
    gates = gate_ref[...]
    moe = None
    for k in range(TOP_K):
        yk = _load_tiles_as_rows(buf_ref, (slot * rows + k * tm) * ROW_CHUNKS, tm)
        term = gates[:, k:k + 1] * yk
        moe = term if moe is None else moe + term
    x2 = x1_ref[...] + moe
    hp = _rms(x2, gple_ref[...]).astype(_BF16)
    x3 = x2 + jax.nn.sigmoid(_dot(hp, wpg_ref[...])) * _dot(p_ref[...].astype(_BF16), wpp_ref[...])
    out_ref[...] = _rms(x3, gfin_ref[...])


def _combine(dest_flat, y_tiles, x1, gates_t, p, g_ple, w_ple_gate, w_ple_proj, g_final):
    T, D = x1.shape
    tm = CMB_TM
    n = T // tm
    row_spec = lambda w: pl.BlockSpec((tm, w), lambda i: (i, 0))
    return pl.pallas_call(
        _combine_kernel,
        grid=(n,),
        in_specs=[
            pl.BlockSpec(memory_space=pl.ANY),
            pl.BlockSpec(memory_space=pl.ANY),
            row_spec(D), row_spec(TOP_K), row_spec(PLE_DIM),
            _const_spec((1, D)), _const_spec((D, D)), _const_spec((PLE_DIM, D)), _const_spec((1, D)),
        ],
        out_specs=row_spec(D),
        out_shape=jax.ShapeDtypeStruct((T, D), _F32),
        scratch_shapes=[
            pltpu.VMEM((2 * TOP_K * tm * ROW_CHUNKS, LANES), _F32),
            pltpu.SMEM((2 * TOP_K * tm,), jnp.int32),
            pltpu.SemaphoreType.DMA((2,)),
            pltpu.SemaphoreType.DMA((2,)),
        ],
        compiler_params=pltpu.CompilerParams(
            dimension_semantics=("arbitrary",), vmem_limit_bytes=VMEM_LIMIT),
        name="combine",
    )(dest_flat, y_tiles, x1, gates_t, p, g_ple.reshape(1, D), w_ple_gate.astype(_BF16),
      w_ple_proj.astype(_BF16), g_final.reshape(1, D))


def kernel(x, p, g_mix, w_in, attn_sinks, g_sgu, b_sgu, w_spatial, b_spatial, w_attn_proj, w_sgu_proj, w_out, g_ffn, w_router, b_router, w_gate, b_gate, w_up, b_up, w_down, b_down, g_ple, w_ple_gate, w_ple_proj, g_final):
    B, S, D = x.shape
    T = B * S
    depth = w_in.shape[0]
    assert depth == 1 and S % MIX_TM == 0 and T % CMB_TM == 0
    tm = EXP_TM
    n_blocks = (T * TOP_K) // tm + N_EXPERTS
    capacity = n_blocks * tm
    for i in range(depth):
        x1, h2_tiles, idx, gates, pos, cnt = _mixer(
            x, g_mix[i], w_in[i], attn_sinks[i], g_sgu[i], b_sgu[i], w_spatial[i], b_spatial[i],
            w_attn_proj[i], w_sgu_proj[i], w_out[i], g_ffn[i], w_router[i], b_router[i])

        counts = cnt[:, 0].astype(jnp.int32)
        padded = (counts + tm - 1) // tm * tm
        pend = jnp.cumsum(padded)
        pstart = pend - padded
        dest = pstart[idx] + pos
        tok_ids = jnp.broadcast_to(jnp.arange(T, dtype=jnp.int32)[None, :], (TOP_K, T))
        row_tok = jnp.zeros((capacity,), jnp.int32).at[dest.reshape(-1)].set(tok_ids.reshape(-1))
        n_used = (pend[-1] // tm).astype(jnp.int32).reshape(1)
        block_start = jnp.arange(n_blocks, dtype=jnp.int32) * tm
        block_e = jnp.searchsorted(pend, block_start, side='right').astype(jnp.int32)
        last_e = jnp.searchsorted(pend, pend[-1] - 1, side='right').astype(jnp.int32)
        block_e = jnp.minimum(block_e, last_e)

        y_tiles = _experts(block_e, n_used, row_tok, h2_tiles.reshape(T, ROW_CHUNKS, LANES),
                           w_gate[i], b_gate[i], w_up[i], b_up[i], w_down[i], b_down[i], n_blocks)

        dest_flat = dest.reshape(TOP_K, T // CMB_TM, CMB_TM).transpose(1, 0, 2).reshape(-1)
        out = _combine(dest_flat, y_tiles.reshape(capacity, ROW_CHUNKS, LANES), x1.reshape(T, D),
                       gates.T, p[i].reshape(T, PLE_DIM), g_ple[i], w_ple_gate[i], w_ple_proj[i], g_final)
    return out.reshape(B, S, D)
```
